```python
import math
import jax, jax.numpy as jnp
from jax import lax
import numpy as np

D_MODEL = 1024
BATCH = 16
SEQ = 4096
DEPTH = 1

PLE_DIM = 256
D_FF = 2816
D_MIX = D_MODEL
RET_WIDTH = D_MIX // 2
RET_HEADS = 8
RET_HEAD_DIM = RET_WIDTH // RET_HEADS
CONV_WIDTH = D_MIX - RET_WIDTH
CONV_KERNEL = 31
CHUNK = 128
ROPE_BASE = 10000.0
EPS = 1e-6
D_IN_PROJ = 4 * RET_WIDTH + 2 * CONV_WIDTH

kernel_name = "hybrid_retention_conformerconv_macaron_layer"


def _rmsnorm(x, g):
    xf = x.astype(jnp.float32)
    y = xf * lax.rsqrt(jnp.mean(xf * xf, axis=-1, keepdims=True) + EPS)
    return (y * g.astype(jnp.float32)).astype(x.dtype)


def _layernorm(x, g, b):
    xf = x.astype(jnp.float32)
    mu = jnp.mean(xf, axis=-1, keepdims=True)
    xc = xf - mu
    y = xc * lax.rsqrt(jnp.mean(xc * xc, axis=-1, keepdims=True) + EPS)
    return (y * g.astype(jnp.float32) + b.astype(jnp.float32)).astype(x.dtype)


def _swiglu(x, w_gu, w_down):
    gu = x @ w_gu
    g, u = jnp.split(gu, 2, axis=-1)
    return (jax.nn.silu(g) * u) @ w_down


def _rotary(x, cos, sin):
    x1, x2 = jnp.split(x, 2, axis=-1)
    c = cos[None, :, None, :].astype(x.dtype)
    s = sin[None, :, None, :].astype(x.dtype)
    return jnp.concatenate([x1 * c - x2 * s, x1 * s + x2 * c], axis=-1)


def _retention_chunkwise(q, k, v):
    B_, S_, H, d = q.shape
    N = S_ // CHUNK
    dt = q.dtype
    log_gamma = jnp.log1p(-jnp.exp2(-5.0 - jnp.arange(H, dtype=jnp.float32)))
    pos = jnp.arange(CHUNK, dtype=jnp.float32)
    diff = pos[:, None] - pos[None, :]
    intra_decay = jnp.where(diff[None] >= 0,
                            jnp.exp(log_gamma[:, None, None] * jnp.maximum(diff, 0.0)[None]),
                            0.0).astype(dt)
    k_decay = jnp.exp(log_gamma[None, :] * (CHUNK - 1.0 - pos)[:, None]).astype(dt)
    q_decay = jnp.exp(log_gamma[None, :] * (pos + 1.0)[:, None]).astype(dt)
    chunk_decay = jnp.exp(log_gamma * CHUNK).astype(dt)

    qc = q.reshape(B_, N, CHUNK, H, d)
    kc = k.reshape(B_, N, CHUNK, H, d)
    vc = v.reshape(B_, N, CHUNK, H, d)

    scores = jnp.einsum('bnihd,bnjhd->bnhij', qc, kc) * intra_decay[None, None]
    intra = jnp.einsum('bnhij,bnjhd->bnihd', scores, vc)

    kv = jnp.einsum('bnjhd,bnjhe->bnhde', kc * k_decay[None, None, :, :, None], vc)

    def step(R, kv_n):
        return R * chunk_decay[None, :, None, None] + kv_n, R

    R0 = jnp.zeros((B_, H, d, d), dtype=kv.dtype)
    _, R_prev = lax.scan(step, R0, jnp.moveaxis(kv, 1, 0))
    R_prev = jnp.moveaxis(R_prev, 0, 1)
    cross = jnp.einsum('bnihd,bnhde->bnihe', qc * q_decay[None, None, :, :, None], R_prev)
    return (intra + cross).reshape(B_, S_, H, d)


def _causal_depthwise_conv(x, w, b):
    C = x.shape[-1]
    y = lax.conv_general_dilated(
        x, w[:, None, :].astype(x.dtype), window_strides=(1,),
        padding=[(CONV_KERNEL - 1, 0)],
        dimension_numbers=('NWC', 'WIO', 'NWC'),
        feature_group_count=C)
    return y + b.astype(x.dtype)


def setup_inputs(seed: int = 0) -> dict:
    key = jax.random.key(seed)
    ks = iter(jax.random.split(key, 40))

    def w(shape, fan_in):
        return jax.random.normal(next(ks), shape, jnp.float32) * (fan_in ** -0.5)

    def gain(shape):
        return 1.0 + 0.1 * jax.random.normal(next(ks), shape, jnp.float32)

    def bias(shape):
        return 0.02 * jax.random.normal(next(ks), shape, jnp.float32)

    L = DEPTH
    return {
        "x": jax.random.normal(next(ks), (BATCH, SEQ, D_MODEL), jnp.float32),
        "p": jax.random.normal(next(ks), (DEPTH, BATCH, SEQ, PLE_DIM), jnp.float32),
        "ffn1_pre_g": gain((L, D_MODEL)),
        "ffn1_w_gu": w((L, D_MODEL, 2 * D_FF), D_MODEL),
        "ffn1_w_down": w((L, D_FF, D_MODEL), D_FF),
        "ffn1_post_g": gain((L, D_MODEL)),
        "mix_pre_g": gain((L, D_MODEL)),
        "w_in": w((L, D_MODEL, D_IN_PROJ), D_MODEL),
        "ret_gn_g": gain((L, RET_WIDTH)),
        "ret_gn_b": bias((L, RET_WIDTH)),
        "conv_w": w((L, CONV_KERNEL, CONV_WIDTH), CONV_KERNEL),
        "conv_b": bias((L, CONV_WIDTH)),
        "conv_ln_g": gain((L, CONV_WIDTH)),
        "conv_ln_b": bias((L, CONV_WIDTH)),
        "w_out": w((L, D_MIX, D_MODEL), D_MIX),
        "mix_post_g": gain((L, D_MODEL)),
        "ffn2_pre_g": gain((L, D_MODEL)),
        "ffn2_w_gu": w((L, D_MODEL, 2 * D_FF), D_MODEL),
        "ffn2_w_down": w((L, D_FF, D_MODEL), D_FF),
        "ffn2_post_g": gain((L, D_MODEL)),
        "ple_w": w((L, PLE_DIM, D_MODEL), PLE_DIM),
        "ple_gate_norm_g": gain((L, D_MODEL)),
        "ple_gate_w": w((L, D_MODEL, D_MODEL), D_MODEL),
        "ple_post_g": gain((L, D_MODEL)),
    }


def reference(x, p, ffn1_pre_g, ffn1_w_gu, ffn1_w_down, ffn1_post_g,
              mix_pre_g, w_in, ret_gn_g, ret_gn_b, conv_w, conv_b, conv_ln_g, conv_ln_b,
              w_out, mix_post_g,
              ffn2_pre_g, ffn2_w_gu, ffn2_w_down, ffn2_post_g,
              ple_w, ple_gate_norm_g, ple_gate_w, ple_post_g):
    B_, S_, _ = x.shape
    inv_freq = ROPE_BASE ** (-jnp.arange(0, RET_HEAD_DIM, 2, dtype=jnp.float32) / RET_HEAD_DIM)
    ang = jnp.arange(S_, dtype=jnp.float32)[:, None] * inv_freq[None, :]
    cos, sin = jnp.cos(ang), jnp.sin(ang)
    q_scale = RET_HEAD_DIM ** -0.5

    h = x
    for i in range(DEPTH):
        f = _swiglu(_rmsnorm(h, ffn1_pre_g[i]), ffn1_w_gu[i], ffn1_w_down[i])
        h = h + 0.5 * _rmsnorm(f, ffn1_post_g[i])

        u = _rmsnorm(h, mix_pre_g[i])
        proj = u @ w_in[i]
        q, k, v, g_ret, glu_a, glu_b = jnp.split(
            proj, [RET_WIDTH, 2 * RET_WIDTH, 3 * RET_WIDTH, 4 * RET_WIDTH,
                   4 * RET_WIDTH + CONV_WIDTH], axis=-1)

        q = _rotary(q.reshape(B_, S_, RET_HEADS, RET_HEAD_DIM), cos, sin) * q_scale
        k = _rotary(k.reshape(B_, S_, RET_HEADS, RET_HEAD_DIM), cos, sin)
        v = v.reshape(B_, S_, RET_HEADS, RET_HEAD_DIM)
        o = _retention_chunkwise(q, k, v)
        o = _layernorm(o, ret_gn_g[i].reshape(RET_HEADS, RET_HEAD_DIM),
                       ret_gn_b[i].reshape(RET_HEADS, RET_HEAD_DIM))
        ret_out = jax.nn.silu(g_ret) * o.reshape(B_, S_, RET_WIDTH)

        c = glu_a * jax.nn.sigmoid(glu_b)
        c = _causal_depthwise_conv(c, conv_w[i], conv_b[i])
        conv_out = jax.nn.silu(_layernorm(c, conv_ln_g[i], conv_ln_b[i]))

        mixed = jnp.concatenate([ret_out, conv_out], axis=-1) @ w_out[i]
        h = h + _rmsnorm(mixed, mix_post_g[i])

        f = _swiglu(_rmsnorm(h, ffn2_pre_g[i]), ffn2_w_gu[i], ffn2_w_down[i])
        h = h + 0.5 * _rmsnorm(f, ffn2_post_g[i])

        e = p[i] @ ple_w[i]
        gate = jax.nn.sigmoid(_rmsnorm(h, ple_gate_norm_g[i]) @ ple_gate_w[i])
        h = h + _rmsnorm(gate * e, ple_post_g[i])
    return h
```

```python
import functools
import math

import jax
import jax.numpy as jnp
from jax import lax
from jax.experimental import pallas as pl
from jax.experimental.pallas import tpu as pltpu

EPS = 1e-6
ROPE_BASE = 10000.0
RET_HEADS = 8
CONV_KERNEL = 31
CHUNK = 128
QUAD = 4
HALO = 32
SUBLANES = 8
CONV_ROWS = 32
V7X_VMEM_LIMIT = 56 * 1024 * 1024

BF16 = jnp.bfloat16
F32 = jnp.float32


def _rms(x, g):
    ms = jnp.mean(x * x, axis=-1, keepdims=True)
    return x * lax.rsqrt(ms + EPS) * g


def _dot(a, b):
    return jnp.dot(a, b, preferred_element_type=F32)


def _const_spec(shape):
    nd = len(shape)
    return pl.BlockSpec(shape, lambda *_: (0,) * nd, pipeline_mode=pl.Buffered(1))


def _ff_chunks(d_ff):
    step = 1024
    out, c = [], 0
    while c < d_ff:
        out.append((c, min(step, d_ff - c)))
        c += step
    return out


def _ffn_core(x, preg_ref, wgu_ref, wd_ref, postg_ref, d_ff):
    xn = _rms(x, preg_ref[...]).astype(BF16)
    f = None
    for c0, fc in _ff_chunks(d_ff):
        g = _dot(xn, wgu_ref[:, c0:c0 + fc])
        u = _dot(xn, wgu_ref[:, d_ff + c0:d_ff + c0 + fc])
        a = (g * jax.nn.sigmoid(g) * u).astype(BF16)
        d = _dot(a, wd_ref[c0:c0 + fc, :])
        f = d if f is None else f + d
    return x + 0.5 * _rms(f, postg_ref[...])


def _ffn_kernel(x_ref, preg_ref, wgu_ref, wd_ref, postg_ref, o_ref, *, d_ff):
    o_ref[...] = _ffn_core(x_ref[...], preg_ref, wgu_ref, wd_ref, postg_ref, d_ff)


def _ffn_ple_kernel(x_ref, p_ref, preg_ref, wgu_ref, wd_ref, postg_ref,
                    plew_ref, gng_ref, gw_ref, plepost_ref, o_ref, *, d_ff):
    h = _ffn_core(x_ref[...], preg_ref, wgu_ref, wd_ref, postg_ref, d_ff)
    e = _dot(p_ref[...].astype(BF16), plew_ref[...])
    gate = jax.nn.sigmoid(_dot(_rms(h, gng_ref[...]).astype(BF16), gw_ref[...]))
    o_ref[...] = h + _rms(gate * e, plepost_ref[...])


def _ffn_call(x2, pre_g, w_gu, w_down, post_g, tm, ple=None):
    t, d = x2.shape
    d_ff = w_down.shape[0]
    row = lambda i: (i, 0)
    args = [x2]
    specs = [pl.BlockSpec((tm, d), row)]
    if ple is not None:
        p2, ple_w, gn_g, gate_w, ple_post_g = ple
        args.append(p2)
        specs.append(pl.BlockSpec((tm, p2.shape[1]), row))
    consts = [pre_g.reshape(1, d), w_gu.astype(BF16), w_down.astype(BF16), post_g.reshape(1, d)]
    if ple is not None:
        consts += [ple_w.astype(BF16), gn_g.reshape(1, d), gate_w.astype(BF16),
                   ple_post_g.reshape(1, d)]
    args += consts
    specs += [_const_spec(c.shape) for c in consts]
    body = _ffn_kernel if ple is None else _ffn_ple_kernel
    return pl.pallas_call(
        functools.partial(body, d_ff=d_ff),
        out_shape=jax.ShapeDtypeStruct((t, d), F32),
        grid=(t // tm,),
        in_specs=specs,
        out_specs=pl.BlockSpec((tm, d), row),
        compiler_params=pltpu.CompilerParams(
            dimension_semantics=("arbitrary",), vmem_limit_bytes=V7X_VMEM_LIMIT),
        name="ffn" if ple is None else "ffn_ple",
    )(*args)


def _conv_taps():
    taps = []
    for j in range(CONV_KERNEL):
        off = HALO - (CONV_KERNEL - 1) + j
        taps.append((j, off % SUBLANES, off - off % SUBLANES))
    return taps


def _mixer_kernel(h_ref, preg_ref, win_ref, cos_ref, sin_ref, hmk_ref, hmv_ref,
                  dtab_ref, qdec_ref, kdec_ref, cd_ref, bd_ref, amean_ref,
                  gng_ref, gnb_ref, convw_ref, convb_ref, lng_ref, lnb_ref,
                  wout_ref, postg_ref, o_ref,
                  r_ref, cbuf_ref, sh_ref, oret_ref, y_ref, mix_ref,
                  *, ts, rw, q_scale):
    cw = rw
    qw = rw // 2
    nquad = 2

    @pl.when(pl.program_id(1) == 0)
    def _():
        r_ref[...] = jnp.zeros_like(r_ref)
        cbuf_ref[0:HALO, :] = jnp.zeros((HALO, cw), F32)

    u = _rms(h_ref[...], preg_ref[...]).astype(BF16)
    proj = _dot(u, win_ref[...])
    cos = cos_ref[...]
    sin = sin_ref[...]

    for qd in range(nquad):
        b0 = qd * qw
        q1 = proj[:, b0:b0 + qw // 2]
        q2 = proj[:, b0 + qw // 2:b0 + qw]
        k1 = proj[:, rw + b0:rw + b0 + qw // 2]
        k2 = proj[:, rw + b0 + qw // 2:rw + b0 + qw]
        qr = jnp.concatenate([q1 * cos - q2 * sin, q1 * sin + q2 * cos], axis=1) * q_scale
        kr = jnp.concatenate([k1 * cos - k2 * sin, k1 * sin + k2 * cos], axis=1)
        v = proj[:, 2 * rw + b0:2 * rw + b0 + qw]
        dtab = dtab_ref[qd]
        qdec = qdec_ref[qd]
        kdec = kdec_ref[qd]
        for c in range(ts // CHUNK):
            r0 = c * CHUNK
            qc = qr[r0:r0 + CHUNK]
            kc = kr[r0:r0 + CHUNK]
            vc = v[r0:r0 + CHUNK]
            vb = vc.astype(BF16)
            kstack = jnp.concatenate(
                [kc * hmk_ref[hh:hh + 1, :] for hh in range(QUAD)], axis=0).astype(BF16)
            vstack = jnp.concatenate(
                [vc * hmv_ref[hh:hh + 1, :] for hh in range(QUAD)], axis=0).astype(BF16)
            s_all = lax.dot_general(qc.astype(BF16), kstack, (((1,), (1,)), ((), ())),
                                    preferred_element_type=F32)
            pmat = (s_all * dtab).astype(BF16)
            r_prev = r_ref[qd]
            lhs = jnp.concatenate([pmat, (qc * qdec).astype(BF16)], axis=1)
            rhs = jnp.concatenate([vstack, r_prev.astype(BF16)], axis=0)
            oret_ref[r0:r0 + CHUNK, b0:b0 + qw] = _dot(lhs, rhs)
            kdt = (kc * kdec).T.astype(BF16)
            kv = _dot(kdt, vb)
            r_ref[qd] = r_prev * cd_ref[qd] + kv * bd_ref[...]

    amean = amean_ref[...]

    def _gmean(a):
        hi = a.astype(BF16)
        lo = (a - hi.astype(F32)).astype(BF16)
        return _dot(hi, amean) + _dot(lo, amean)

    for qd in range(nquad):
        b0 = qd * qw
        o = oret_ref[:, b0:b0 + qw]
        xc = o - _gmean(o)
        var = _gmean(xc * xc)
        yn = xc * lax.rsqrt(var + EPS) * gng_ref[:, b0:b0 + qw] + gnb_ref[:, b0:b0 + qw]
        gr = proj[:, 3 * rw + b0:3 * rw + b0 + qw]
        mix_ref[:, b0:b0 + qw] = (gr * jax.nn.sigmoid(gr) * yn).astype(BF16)

    ga = proj[:, 4 * rw:4 * rw + cw]
    gb = proj[:, 4 * rw + cw:4 * rw + 2 * cw]
    cbuf_ref[HALO:HALO + ts, :] = ga * jax.nn.sigmoid(gb)
    sh_rows = ts + HALO - SUBLANES
    for r in range(1, SUBLANES):
        sh_ref[r - 1] = cbuf_ref[r:r + sh_rows, :]
    taps = _conv_taps()
    bias = convb_ref[...]

    def conv_block(i, carry):
        t0 = pl.multiple_of(i * CONV_ROWS, CONV_ROWS)
        acc = jnp.broadcast_to(bias, (CONV_ROWS, cw))
        for j, r, a in taps:
            w = convw_ref[j]
            w = jnp.concatenate([w] * (CONV_ROWS // SUBLANES), axis=0)
            if r == 0:
                xs = cbuf_ref[pl.ds(t0 + a, CONV_ROWS), :]
            else:
                xs = sh_ref[r - 1, pl.ds(t0 + a, CONV_ROWS), :]
            acc = acc + xs * w
        y_ref[pl.ds(t0, CONV_ROWS), :] = acc
        return carry

    lax.fori_loop(0, ts // CONV_ROWS, conv_block, 0)
    cbuf_ref[0:HALO, :] = cbuf_ref[ts:ts + HALO, :]

    y = y_ref[...]
    mu = jnp.mean(y, axis=-1, keepdims=True)
    yc = y - mu
    yn = yc * lax.rsqrt(jnp.mean(yc * yc, axis=-1, keepdims=True) + EPS) * lng_ref[...] + lnb_ref[...]
    mix_ref[:, rw:rw + cw] = (yn * jax.nn.sigmoid(yn)).astype(BF16)

    mixed = _dot(mix_ref[...], wout_ref[...])
    o_ref[...] = h_ref[...] + _rms(mixed, postg_ref[...])


def _mixer_tables(s_len, rw, hd):
    heads = rw // hd
    half = hd // 2
    qw = QUAD * hd
    inv_freq = ROPE_BASE ** (-jnp.arange(0, hd, 2, dtype=F32) / hd)
    ang = jnp.arange(s_len, dtype=F32)[:, None] * inv_freq[None, :]
    cos = jnp.tile(jnp.cos(ang), (1, QUAD))
    sin = jnp.tile(jnp.sin(ang), (1, QUAD))

    log_gamma = jnp.log1p(-jnp.exp2(-5.0 - jnp.arange(heads, dtype=F32)))
    pos = jnp.arange(CHUNK, dtype=F32)
    diff = pos[:, None] - pos[None, :]
    intra = jnp.where(diff[None] >= 0,
                      jnp.exp(log_gamma[:, None, None] * jnp.maximum(diff, 0.0)[None]), 0.0)
    k_decay = jnp.exp(log_gamma[None, :] * (CHUNK - 1.0 - pos)[:, None])
    q_decay = jnp.exp(log_gamma[None, :] * (pos + 1.0)[:, None])
    chunk_decay = jnp.exp(log_gamma * CHUNK)

    nquad = heads // QUAD
    lane = jnp.arange(qw)
    hk = (lane % (QUAD * half)) // half
    hv = lane // hd
    hid = jnp.arange(QUAD)
    hmk = (hk[None, :] == hid[:, None]).astype(F32)
    hmv = (hv[None, :] == hid[:, None]).astype(F32)
    dtab = intra.reshape(nquad, QUAD, CHUNK, CHUNK).transpose(0, 2, 1, 3).reshape(
        nquad, CHUNK, QUAD * CHUNK)
    qh = jnp.arange(nquad)[:, None] * QUAD + hk[None, :]
    qdec = jnp.transpose(q_decay[:, qh], (1, 0, 2))
    kdec = jnp.transpose(k_decay[:, qh], (1, 0, 2))
    bd = (hk[:, None] == hv[None, :]).astype(F32)
    cd = chunk_decay[qh][:, :, None] * bd[None]
    amean = ((hv[:, None] == hv[None, :]).astype(F32) / hd).astype(BF16)
    return cos, sin, hmk, hmv, dtab, qdec, kdec, cd, bd, amean


def _qk_perm(rw, hd):
    half = hd // 2
    qw = QUAD * hd
    c = jnp.arange(rw)
    quad, w = c // qw, c % qw
    part, hh, e = w // (QUAD * half), (w % (QUAD * half)) // half, w % half
    return (quad * QUAD + hh) * hd + part * half + e


def _mixer_call(h2, bsz, s_len, pre_g, w_in, gn_g, gn_b, conv_w, conv_b, ln_g, ln_b,
                w_out, post_g, ts):
    t, d = h2.shape
    rw = gn_g.shape[0]
    cw = conv_b.shape[0]
    assert rw == cw and rw % (QUAD * 2) == 0
    hd = rw // RET_HEADS
    assert ts % CHUNK == 0 and s_len % ts == 0 and ts % CONV_ROWS == 0
    nst = s_len // ts
    tabs = _mixer_tables(s_len, rw, hd)
    cos, sin = tabs[0], tabs[1]
    perm = _qk_perm(rw, hd)
    w_in_p = jnp.concatenate(
        [w_in[:, :rw][:, perm], w_in[:, rw:2 * rw][:, perm], w_in[:, 2 * rw:]], axis=1).astype(BF16)
    convw = jnp.broadcast_to(conv_w[:, None, :], (CONV_KERNEL, SUBLANES, cw))
    consts_a = [pre_g.reshape(1, d), w_in_p]
    consts_b = list(tabs[2:]) + [gn_g.reshape(1, rw), gn_b.reshape(1, rw), convw,
                                 conv_b.reshape(1, cw), ln_g.reshape(1, cw), ln_b.reshape(1, cw),
                                 w_out.astype(BF16), post_g.reshape(1, d)]
    row = lambda b, s: (b * nst + s, 0)
    tab = lambda b, s: (s, 0)
    specs = ([pl.BlockSpec((ts, d), row)] + [_const_spec(c.shape) for c in consts_a]
             + [pl.BlockSpec((ts, cos.shape[1]), tab), pl.BlockSpec((ts, sin.shape[1]), tab)]
             + [_const_spec(c.shape) for c in consts_b])
    qw = QUAD * hd
    scratch = [
        pltpu.VMEM((rw // qw, qw, qw), F32),
        pltpu.VMEM((ts + HALO, cw), F32),
        pltpu.VMEM((SUBLANES - 1, ts + HALO - SUBLANES, cw), F32),
        pltpu.VMEM((ts, rw), F32),
        pltpu.VMEM((ts, cw), F32),
        pltpu.VMEM((ts, rw + cw), BF16),
    ]
    return pl.pallas_call(
        functools.partial(_mixer_kernel, ts=ts, rw=rw, q_scale=hd ** -0.5),
        out_shape=jax.ShapeDtypeStruct((t, d), F32),
        grid=(bsz, nst),
        in_specs=specs,
        out_specs=pl.BlockSpec((ts, d), row),
        scratch_shapes=scratch,
        compiler_params=pltpu.CompilerParams(
            dimension_semantics=("arbitrary", "arbitrary"), vmem_limit_bytes=V7X_VMEM_LIMIT),
        name="mixer",
    )(h2, *consts_a, cos, sin, *consts_b)


def _tile(n, want):
    t = min(n, want)
    while n % t:
        t //= 2
    return t


def kernel(x, p, ffn1_pre_g, ffn1_w_gu, ffn1_w_down, ffn1_post_g, mix_pre_g, w_in, ret_gn_g, ret_gn_b, conv_w, conv_b, conv_ln_g, conv_ln_b, w_out, mix_post_g, ffn2_pre_g, ffn2_w_gu, ffn2_w_down, ffn2_post_g, ple_w, ple_gate_norm_g, ple_gate_w, ple_post_g):
    bsz, s_len, d = x.shape
    t = bsz * s_len
    tm = _tile(t, 512)
    ts = _tile(s_len, 512)
    h = x.reshape(t, d)
    for i in range(p.shape[0]):
        h = _ffn_call(h, ffn1_pre_g[i], ffn1_w_gu[i], ffn1_w_down[i], ffn1_post_g[i], tm)
        h = _mixer_call(h, bsz, s_len, mix_pre_g[i], w_in[i], ret_gn_g[i], ret_gn_b[i],
                        conv_w[i], conv_b[i], conv_ln_g[i], conv_ln_b[i], w_out[i],
                        mix_post_g[i], ts)
        h = _ffn_call(h, ffn2_pre_g[i], ffn2_w_gu[i], ffn2_w_down[i], ffn2_post_g[i], tm,
                      ple=(p[i].reshape(t, p.shape[-1]), ple_w[i], ple_gate_norm_g[i],
                           ple_gate_w[i], ple_post_g[i]))
    return h.reshape(bsz, s_len, d)
```

```python
import functools

import jax
import jax.numpy as jnp
from jax import lax
from jax.experimental import pallas as pl
from jax.experimental.pallas import tpu as pltpu

EPS = 1e-6
ROPE_BASE = 10000.0
RET_HEADS = 8
CONV_KERNEL = 31
CHUNK = 128
QUAD = 4
SUBLANES = 8
LANES = 128
CONV_TILES = 8
V7X_VMEM_LIMIT = 56 * 1024 * 1024

BF16 = jnp.bfloat16
F32 = jnp.float32


def _rms(x, g):
    ms = jnp.mean(x * x, axis=-1, keepdims=True)
    return x * lax.rsqrt(ms + EPS) * g


def _dot(a, b):
    return jnp.dot(a, b, preferred_element_type=F32)


def _const_spec(shape):
    nd = len(shape)
    return pl.BlockSpec(shape, lambda *_: (0,) * nd, pipeline_mode=pl.Buffered(1))


def _ff_chunks(d_ff):
    step = 1024
    out, c = [], 0
    while c < d_ff:
        out.append((c, min(step, d_ff - c)))
        c += step
    return out


def _ffn_core(x, preg_ref, wgu_ref, wd_ref, postg_ref, d_ff):
    xn = _rms(x, preg_ref[...]).astype(BF16)
    f = None
    for c0, fc in _ff_chunks(d_ff):
        g = _dot(xn, wgu_ref[:, c0:c0 + fc])
        u = _dot(xn, wgu_ref[:, d_ff + c0:d_ff + c0 + fc])
        a = (g * jax.nn.sigmoid(g) * u).astype(BF16)
        d = _dot(a, wd_ref[c0:c0 + fc, :])
        f = d if f is None else f + d
    return x + 0.5 * _rms(f, postg_ref[...])


def _ffn_kernel(x_ref, preg_ref, wgu_ref, wd_ref, postg_ref, o_ref, *, d_ff):
    o_ref[...] = _ffn_core(x_ref[...], preg_ref, wgu_ref, wd_ref, postg_ref, d_ff)


def _ffn_ple_kernel(x_ref, p_ref, preg_ref, wgu_ref, wd_ref, postg_ref,
                    plew_ref, gng_ref, gw_ref, plepost_ref, o_ref, *, d_ff):
    h = _ffn_core(x_ref[...], preg_ref, wgu_ref, wd_ref, postg_ref, d_ff)
    e = _dot(p_ref[...].astype(BF16), plew_ref[...])
    gate = jax.nn.sigmoid(_dot(_rms(h, gng_ref[...]).astype(BF16), gw_ref[...]))
    o_ref[...] = h + _rms(gate * e, plepost_ref[...])


def _ffn_call(x2, pre_g, w_gu, w_down, post_g, tm, ple=None):
    t, d = x2.shape
    d_ff = w_down.shape[0]
    row = lambda i: (i, 0)
    args = [x2]
    specs = [pl.BlockSpec((tm, d), row)]
    if ple is not None:
        p2, ple_w, gn_g, gate_w, ple_post_g = ple
        args.append(p2)
        specs.append(pl.BlockSpec((tm, p2.shape[1]), row))
    consts = [pre_g.reshape(1, d), w_gu.astype(BF16), w_down.astype(BF16), post_g.reshape(1, d)]
    if ple is not None:
        consts += [ple_w.astype(BF16), gn_g.reshape(1, d), gate_w.astype(BF16),
                   ple_post_g.reshape(1, d)]
    args += consts
    specs += [_const_spec(c.shape) for c in consts]
    body = _ffn_kernel if ple is None else _ffn_ple_kernel
    return pl.pallas_call(
        functools.partial(body, d_ff=d_ff),
        out_shape=jax.ShapeDtypeStruct((t, d), F32),
        grid=(t // tm,),
        in_specs=specs,
        out_specs=pl.BlockSpec((tm, d), row),
        compiler_params=pltpu.CompilerParams(
            dimension_semantics=("arbitrary",), vmem_limit_bytes=V7X_VMEM_LIMIT),
        name="ffn" if ple is None else "ffn_ple",
    )(*args)


def _conv_hist(ts):
    hist = -(-(CONV_KERNEL - 1) // SUBLANES) * SUBLANES
    while ((hist + ts) // SUBLANES) % 2 == 0:
        hist += SUBLANES
    return hist


def _conv(c, b_ref, xp_ref, ys_ref, convw_ref, convb_ref, *, ts, cw):
    hist = _conv_hist(ts)
    seg = (hist + ts) // SUBLANES
    pad = CONV_KERNEL - 1
    ncol = cw // LANES
    for col in range(ncol):
        b_ref[col, hist:hist + ts, :] = c[:, col * LANES:(col + 1) * LANES]
    for col in range(ncol):
        for i in range(seg):
            xp_ref[col, (pad + i) * SUBLANES:(pad + i + 1) * SUBLANES, :] = (
                b_ref[col, pl.ds(i, SUBLANES, stride=seg), :])
        for k in range(1, pad + 1):
            src = xp_ref[col, (pad + seg - k) * SUBLANES:(pad + seg - k + 1) * SUBLANES, :]
            xp_ref[col, (pad - k) * SUBLANES:(pad - k + 1) * SUBLANES, :] = pltpu.roll(src, 1, axis=0)
    for col in range(ncol):
        lanes = slice(col * LANES, (col + 1) * LANES)
        bias = convb_ref[:, lanes]
        for i0 in range(0, seg, CONV_TILES):
            g = min(CONV_TILES, seg - i0)
            acc = None
            for j in range(CONV_KERNEL):
                lo = (pad + i0 - (CONV_KERNEL - 1 - j)) * SUBLANES
                w = jnp.concatenate([convw_ref[j, :, lanes]] * g, axis=0)
                t = xp_ref[col, lo:lo + g * SUBLANES, :] * w
                acc = t if acc is None else acc + t
            acc = acc + bias
            for i in range(g):
                ys_ref[col, pl.ds(i0 + i, SUBLANES, stride=seg), :] = acc[i * SUBLANES:(i + 1) * SUBLANES]
    for col in range(ncol):
        b_ref[col, 0:hist, :] = b_ref[col, ts:ts + hist, :]
    return jnp.concatenate([ys_ref[col, hist:hist + ts, :] for col in range(ncol)], axis=1)


def _mixer_kernel(h_ref, preg_ref, win_ref, cos_ref, sin_ref, hmk_ref, hmv_ref,
                  dtab_ref, qdec_ref, kdec_ref, cd_ref, bd_ref, amean_ref,
                  gng_ref, gnb_ref, convw_ref, convb_ref, lng_ref, lnb_ref,
                  wout_ref, postg_ref, o_ref,
                  r_ref, b_ref, xp_ref, ys_ref, oret_ref, mix_ref,
                  *, ts, rw):
    cw = rw
    qw = rw // 2
    nquad = 2

    @pl.when(pl.program_id(1) == 0)
    def _():
        r_ref[...] = jnp.zeros_like(r_ref)
        b_ref[...] = jnp.zeros_like(b_ref)

    u = _rms(h_ref[...], preg_ref[...]).astype(BF16)

    pg = _dot(u, win_ref[:, 0:2 * cw])
    c = pg[:, 0:cw] * jax.nn.sigmoid(pg[:, cw:2 * cw])
    proj = _dot(u, win_ref[:, 2 * cw:])
    y = _conv(c, b_ref, xp_ref, ys_ref, convw_ref, convb_ref, ts=ts, cw=cw)
    yc = y - jnp.mean(y, axis=-1, keepdims=True)
    yn = yc * lax.rsqrt(jnp.mean(yc * yc, axis=-1, keepdims=True) + EPS) * lng_ref[...] + lnb_ref[...]
    mix_ref[:, rw:rw + cw] = (yn * jax.nn.sigmoid(yn)).astype(BF16)

    cos = cos_ref[...]
    sin = sin_ref[...]
    for qd in range(nquad):
        b0 = qd * qw
        q1 = proj[:, b0:b0 + qw // 2]
        q2 = proj[:, b0 + qw // 2:b0 + qw]
        k1 = proj[:, rw + b0:rw + b0 + qw // 2]
        k2 = proj[:, rw + b0 + qw // 2:rw + b0 + qw]
        qr = jnp.concatenate([q1 * cos - q2 * sin, q1 * sin + q2 * cos], axis=1)
        kr = jnp.concatenate([k1 * cos - k2 * sin, k1 * sin + k2 * cos], axis=1)
        v = proj[:, 2 * rw + b0:2 * rw + b0 + qw]
        dtab = dtab_ref[qd]
        qdec = qdec_ref[qd]
        kdec = kdec_ref[qd]
        state = r_ref[qd]
        for ch in range(ts // CHUNK):
            r0 = ch * CHUNK
            qc = qr[r0:r0 + CHUNK]
            kc = kr[r0:r0 + CHUNK]
            vb = v[r0:r0 + CHUNK].astype(BF16)
            kb = kc.astype(BF16)
            kstack = jnp.concatenate([kb * hmk_ref[hh] for hh in range(QUAD)], axis=0)
            vstack = jnp.concatenate([vb * hmv_ref[hh] for hh in range(QUAD)], axis=0)
            s_all = lax.dot_general(qc.astype(BF16), kstack, (((1,), (1,)), ((), ())),
                                    preferred_element_type=F32)
            pmat = (s_all * dtab).astype(BF16)
            lhs = jnp.concatenate([pmat, (qc * qdec).astype(BF16)], axis=1)
            rhs = jnp.concatenate([vstack, state.astype(BF16)], axis=0)
            oret_ref[r0:r0 + CHUNK, b0:b0 + qw] = _dot(lhs, rhs)
            kdt = (kc * kdec).T.astype(BF16)
            kv = _dot(kdt, vb)
            state = state * cd_ref[qd] + kv * bd_ref[...]
        r_ref[qd] = state

    amean = amean_ref[...]
    for qd in range(nquad):
        b0 = qd * qw
        o = oret_ref[:, b0:b0 + qw]
        hi = o.astype(BF16)
        lo = (o - hi.astype(F32)).astype(BF16)
        xc = o - (_dot(hi, amean) + _dot(lo, amean))
        var = _dot((xc * xc).astype(BF16), amean)
        yn = xc * lax.rsqrt(var + EPS) * gng_ref[:, b0:b0 + qw] + gnb_ref[:, b0:b0 + qw]
        gr = proj[:, 3 * rw + b0:3 * rw + b0 + qw]
        mix_ref[:, b0:b0 + qw] = (gr * jax.nn.sigmoid(gr) * yn).astype(BF16)

    mixed = _dot(mix_ref[...], wout_ref[...])
    o_ref[...] = h_ref[...] + _rms(mixed, postg_ref[...])


def _mixer_tables(s_len, rw, hd, q_scale):
    heads = rw // hd
    half = hd // 2
    qw = QUAD * hd
    inv_freq = ROPE_BASE ** (-jnp.arange(0, hd, 2, dtype=F32) / hd)
    ang = jnp.arange(s_len, dtype=F32)[:, None] * inv_freq[None, :]
    cos = jnp.tile(jnp.cos(ang), (1, QUAD))
    sin = jnp.tile(jnp.sin(ang), (1, QUAD))

    log_gamma = jnp.log1p(-jnp.exp2(-5.0 - jnp.arange(heads, dtype=F32)))
    pos = jnp.arange(CHUNK, dtype=F32)
    diff = pos[:, None] - pos[None, :]
    intra = jnp.where(diff[None] >= 0,
                      jnp.exp(log_gamma[:, None, None] * jnp.maximum(diff, 0.0)[None]), 0.0)
    k_decay = jnp.exp(log_gamma[None, :] * (CHUNK - 1.0 - pos)[:, None])
    q_decay = jnp.exp(log_gamma[None, :] * (pos + 1.0)[:, None])
    chunk_decay = jnp.exp(log_gamma * CHUNK)

    nquad = heads // QUAD
    lane = jnp.arange(qw)
    hk = (lane % (QUAD * half)) // half
    hv = lane // hd
    hid = jnp.arange(QUAD)
    hmk = jnp.broadcast_to((hk[None, :] == hid[:, None]).astype(BF16)[:, None, :],
                           (QUAD, CHUNK, qw))
    hmv = jnp.broadcast_to((hv[None, :] == hid[:, None]).astype(BF16)[:, None, :],
                           (QUAD, CHUNK, qw))
    dtab = q_scale * intra.reshape(nquad, QUAD, CHUNK, CHUNK).transpose(0, 2, 1, 3).reshape(
        nquad, CHUNK, QUAD * CHUNK)
    qh = jnp.arange(nquad)[:, None] * QUAD + hk[None, :]
    qdec = q_scale * jnp.transpose(q_decay[:, qh], (1, 0, 2))
    kdec = jnp.transpose(k_decay[:, qh], (1, 0, 2))
    bd = (hk[:, None] == hv[None, :]).astype(F32)
    cd = chunk_decay[qh][:, :, None] * bd[None]
    amean = ((hv[:, None] == hv[None, :]).astype(F32) / hd).astype(BF16)
    return cos, sin, hmk, hmv, dtab, qdec, kdec, cd, bd, amean


def _qk_perm(rw, hd):
    half = hd // 2
    qw = QUAD * hd
    c = jnp.arange(rw)
    quad, w = c // qw, c % qw
    part, hh, e = w // (QUAD * half), (w % (QUAD * half)) // half, w % half
    return (quad * QUAD + hh) * hd + part * half + e


def _mixer_call(h2, bsz, s_len, pre_g, w_in, gn_g, gn_b, conv_w, conv_b, ln_g, ln_b,
                w_out, post_g, ts):
    t, d = h2.shape
    rw = gn_g.shape[0]
    cw = conv_b.shape[0]
    assert rw == cw and rw % (QUAD * 2) == 0
    hd = rw // RET_HEADS
    assert ts % CHUNK == 0 and s_len % ts == 0
    nst = s_len // ts
    tabs = _mixer_tables(s_len, rw, hd, hd ** -0.5)
    cos, sin = tabs[0], tabs[1]
    perm = _qk_perm(rw, hd)
    w_in_p = jnp.concatenate(
        [w_in[:, 4 * rw:], w_in[:, :rw][:, perm], w_in[:, rw:2 * rw][:, perm], w_in[:, 2 * rw:4 * rw]],
        axis=1).astype(BF16)
    convw = jnp.broadcast_to(conv_w[:, None, :], (CONV_KERNEL, SUBLANES, cw))
    consts_a = [pre_g.reshape(1, d), w_in_p]
    consts_b = list(tabs[2:]) + [gn_g.reshape(1, rw), gn_b.reshape(1, rw), convw,
                                 conv_b.reshape(1, cw), ln_g.reshape(1, cw), ln_b.reshape(1, cw),
                                 w_out.astype(BF16), post_g.reshape(1, d)]
    row = lambda b, s: (b * nst + s, 0)
    tab = lambda b, s: (s, 0)
    specs = ([pl.BlockSpec((ts, d), row)] + [_const_spec(c.shape) for c in consts_a]
             + [pl.BlockSpec((ts, cos.shape[1]), tab), pl.BlockSpec((ts, sin.shape[1]), tab)]
             + [_const_spec(c.shape) for c in consts_b])
    qw = QUAD * hd
    hist = _conv_hist(ts)
    scratch = [
        pltpu.VMEM((rw // qw, qw, qw), F32),
        pltpu.VMEM((cw // LANES, hist + ts, LANES), F32),
        pltpu.VMEM((cw // LANES, hist + ts + (CONV_KERNEL - 1) * SUBLANES, LANES), F32),
        pltpu.VMEM((cw // LANES, hist + ts, LANES), F32),
        pltpu.VMEM((ts, rw), F32),
        pltpu.VMEM((ts, rw + cw), BF16),
    ]
    return pl.pallas_call(
        functools.partial(_mixer_kernel, ts=ts, rw=rw),
        out_shape=jax.ShapeDtypeStruct((t, d), F32),
        grid=(bsz, nst),
        in_specs=specs,
        out_specs=pl.BlockSpec((ts, d), row),
        scratch_shapes=scratch,
        compiler_params=pltpu.CompilerParams(
            dimension_semantics=("arbitrary", "arbitrary"), vmem_limit_bytes=V7X_VMEM_LIMIT),
        name="mixer",
    )(h2, *consts_a, cos, sin, *consts_b)


def _tile(n, want):
    t = min(n, want)
    while n % t:
        t //= 2
    return t


def kernel(x, p, ffn1_pre_g, ffn1_w_gu, ffn1_w_down, ffn1_post_g, mix_pre_g, w_in, ret_gn_g, ret_gn_b, conv_w, conv_b, conv_ln_g, conv_ln_b, w_out, mix_post_g, ffn2_pre_g, ffn2_w_gu, ffn2_w_down, ffn2_post_g, ple_w, ple_gate_norm_g, ple_gate_w, ple_post_g):
    bsz, s_len, d = x.shape
    t = bsz * s_len
    tm = _tile(t, 512)
    ts = _tile(s_len, 512)
    h = x.reshape(t, d)
    for i in range(p.shape[0]):
        h = _ffn_call(h, ffn1_pre_g[i], ffn1_w_gu[i], ffn1_w_down[i], ffn1_post_g[i], tm)
        h = _mixer_call(h, bsz, s_len, mix_pre_g[i], w_in[i], ret_gn_g[i], ret_gn_b[i],
                        conv_w[i], conv_b[i], conv_ln_g[i], conv_ln_b[i], w_out[i],
                        mix_post_g[i], ts)
        h = _ffn_call(h, ffn2_pre_g[i], ffn2_w_gu[i], ffn2_w_down[i], ffn2_post_g[i], tm,
                      ple=(p[i].reshape(t, p.shape[-1]), ple_w[i], ple_gate_norm_g[i],
                           ple_gate_w[i], ple_post_g[i]))
    return h.reshape(bsz, s_len, d)
```

```python
import functools

import jax
import jax.numpy as jnp
from jax import lax
from jax.experimental import pallas as pl
from jax.experimental.pallas import tpu as pltpu

EPS = 1e-6
ROPE_BASE = 10000.0
RET_HEADS = 8
CONV_KERNEL = 31
CHUNK = 128
QUAD = 4
SUBLANES = 8
LANES = 128
CONV_TILES = 8
FFN_ROWS = 512
V7X_VMEM_LIMIT = 56 * 1024 * 1024

BF16 = jnp.bfloat16
F32 = jnp.float32


def _rms(x, g):
    ms = jnp.mean(x * x, axis=-1, keepdims=True)
    return x * lax.rsqrt(ms + EPS) * g


def _dot(a, b):
    return jnp.dot(a, b, preferred_element_type=F32)


def _const_spec(shape):
    nd = len(shape)
    return pl.BlockSpec(shape, lambda *_: (0,) * nd, pipeline_mode=pl.Buffered(1))


def _ff_chunks(d_ff):
    step = 1024
    out, c = [], 0
    while c < d_ff:
        out.append((c, min(step, d_ff - c)))
        c += step
    return out


def _sub_tiles(tm):
    n = max(1, tm // FFN_ROWS)
    return [slice(i * (tm // n), (i + 1) * (tm // n)) for i in range(n)]


def _ffn_core(x, preg_ref, wgu_ref, wd_ref, postg_ref, d_ff):
    xn = _rms(x, preg_ref[...]).astype(BF16)
    f = None
    for c0, fc in _ff_chunks(d_ff):
        g = _dot(xn, wgu_ref[:, c0:c0 + fc])
        u = _dot(xn, wgu_ref[:, d_ff + c0:d_ff + c0 + fc])
        a = (g * jax.nn.sigmoid(g) * u).astype(BF16)
        d = _dot(a, wd_ref[c0:c0 + fc, :])
        f = d if f is None else f + d
    return x + 0.5 * _rms(f, postg_ref[...])


def _ffn_kernel(x_ref, preg_ref, wgu_ref, wd_ref, postg_ref, o_ref, *, d_ff):
    for rows in _sub_tiles(x_ref.shape[0]):
        o_ref[rows, :] = _ffn_core(x_ref[rows, :], preg_ref, wgu_ref, wd_ref, postg_ref, d_ff)


def _ffn_ple_kernel(x_ref, p_ref, preg_ref, wgu_ref, wd_ref, postg_ref,
                    plew_ref, gng_ref, gw_ref, plepost_ref, o_ref, *, d_ff):
    for rows in _sub_tiles(x_ref.shape[0]):
        h = _ffn_core(x_ref[rows, :], preg_ref, wgu_ref, wd_ref, postg_ref, d_ff)
        e = _dot(p_ref[rows, :].astype(BF16), plew_ref[...])
        gate = jax.nn.sigmoid(_dot(_rms(h, gng_ref[...]).astype(BF16), gw_ref[...]))
        o_ref[rows, :] = h + _rms(gate * e, plepost_ref[...])


def _ffn_call(x2, pre_g, w_gu, w_down, post_g, tm, ple=None):
    t, d = x2.shape
    d_ff = w_down.shape[0]
    row = lambda i: (i, 0)
    args = [x2]
    specs = [pl.BlockSpec((tm, d), row)]
    if ple is not None:
        p2, ple_w, gn_g, gate_w, ple_post_g = ple
        args.append(p2)
        specs.append(pl.BlockSpec((tm, p2.shape[1]), row))
    consts = [pre_g.reshape(1, d), w_gu.astype(BF16), w_down.astype(BF16), post_g.reshape(1, d)]
    if ple is not None:
        consts += [ple_w.astype(BF16), gn_g.reshape(1, d), gate_w.astype(BF16),
                   ple_post_g.reshape(1, d)]
    args += consts
    specs += [_const_spec(c.shape) for c in consts]
    body = _ffn_kernel if ple is None else _ffn_ple_kernel
    return pl.pallas_call(
        functools.partial(body, d_ff=d_ff),
        out_shape=jax.ShapeDtypeStruct((t, d), F32),
        grid=(t // tm,),
        in_specs=specs,
        out_specs=pl.BlockSpec((tm, d), row),
        compiler_params=pltpu.CompilerParams(
            dimension_semantics=("arbitrary",), vmem_limit_bytes=V7X_VMEM_LIMIT),
        name="ffn" if ple is None else "ffn_ple",
    )(*args)


def _conv_hist(ts):
    hist = -(-(CONV_KERNEL - 1) // SUBLANES) * SUBLANES
    while ((hist + ts) // SUBLANES) % 2 == 0:
        hist += SUBLANES
    return hist


def _conv(c, b_ref, xp_ref, ys_ref, convw_ref, convb_ref, *, ts, cw):
    hist = _conv_hist(ts)
    seg = (hist + ts) // SUBLANES
    pad = CONV_KERNEL - 1
    ncol = cw // LANES
    for col in range(ncol):
        b_ref[col, hist:hist + ts, :] = c[:, col * LANES:(col + 1) * LANES]
    for col in range(ncol):
        for i in range(seg):
            xp_ref[col, (pad + i) * SUBLANES:(pad + i + 1) * SUBLANES, :] = (
                b_ref[col, pl.ds(i, SUBLANES, stride=seg), :])
        for k in range(1, pad + 1):
            src = xp_ref[col, (pad + seg - k) * SUBLANES:(pad + seg - k + 1) * SUBLANES, :]
            xp_ref[col, (pad - k) * SUBLANES:(pad - k + 1) * SUBLANES, :] = pltpu.roll(src, 1, axis=0)
    for col in range(ncol):
        lanes = slice(col * LANES, (col + 1) * LANES)
        bias = convb_ref[:, lanes]
        for i0 in range(0, seg, CONV_TILES):
            g = min(CONV_TILES, seg - i0)
            acc = None
            for j in range(CONV_KERNEL):
                lo = (pad + i0 - (CONV_KERNEL - 1 - j)) * SUBLANES
                w = jnp.concatenate([convw_ref[j, :, lanes]] * g, axis=0)
                t = xp_ref[col, lo:lo + g * SUBLANES, :] * w
                acc = t if acc is None else acc + t
            acc = acc + bias
            for i in range(g):
                ys_ref[col, pl.ds(i0 + i, SUBLANES, stride=seg), :] = acc[i * SUBLANES:(i + 1) * SUBLANES]
    for col in range(ncol):
        b_ref[col, 0:hist, :] = b_ref[col, ts:ts + hist, :]
    return jnp.concatenate([ys_ref[col, hist:hist + ts, :] for col in range(ncol)], axis=1)


def _mixer_kernel(h_ref, preg_ref, win_ref, cos_ref, sin_ref, hmk_ref, hmv_ref,
                  dtab_ref, qdec_ref, kdec_ref, cd_ref, bd_ref, amean_ref,
                  gng_ref, gnb_ref, convw_ref, convb_ref, lng_ref, lnb_ref,
                  wout_ref, postg_ref, o_ref,
                  r_ref, b_ref, xp_ref, ys_ref, oret_ref, mix_ref,
                  *, ts, rw):
    cw = rw
    qw = rw // 2
    nquad = 2

    @pl.when(pl.program_id(1) == 0)
    def _():
        r_ref[...] = jnp.zeros_like(r_ref)
        b_ref[...] = jnp.zeros_like(b_ref)

    u = _rms(h_ref[...], preg_ref[...]).astype(BF16)

    pg = _dot(u, win_ref[:, 0:2 * cw])
    c = pg[:, 0:cw] * jax.nn.sigmoid(pg[:, cw:2 * cw])
    proj = _dot(u, win_ref[:, 2 * cw:])
    y = _conv(c, b_ref, xp_ref, ys_ref, convw_ref, convb_ref, ts=ts, cw=cw)
    yc = y - jnp.mean(y, axis=-1, keepdims=True)
    yn = yc * lax.rsqrt(jnp.mean(yc * yc, axis=-1, keepdims=True) + EPS) * lng_ref[...] + lnb_ref[...]
    mix_ref[:, rw:rw + cw] = (yn * jax.nn.sigmoid(yn)).astype(BF16)

    cos = cos_ref[...]
    sin = sin_ref[...]
    nch = ts // CHUNK
    qcs, kcs, vstacks, kvs, pmats = {}, {}, {}, {}, {}
    for qd in range(nquad):
        b0 = qd * qw
        q1 = proj[:, b0:b0 + qw // 2]
        q2 = proj[:, b0 + qw // 2:b0 + qw]
        k1 = proj[:, rw + b0:rw + b0 + qw // 2]
        k2 = proj[:, rw + b0 + qw // 2:rw + b0 + qw]
        qr = jnp.concatenate([q1 * cos - q2 * sin, q1 * sin + q2 * cos], axis=1)
        kr = jnp.concatenate([k1 * cos - k2 * sin, k1 * sin + k2 * cos], axis=1)
        v = proj[:, 2 * rw + b0:2 * rw + b0 + qw]
        for ch in range(nch):
            r0 = ch * CHUNK
            qcs[qd, ch] = qr[r0:r0 + CHUNK]
            kcs[qd, ch] = kr[r0:r0 + CHUNK]
            vb = v[r0:r0 + CHUNK].astype(BF16)
            vstacks[qd, ch] = jnp.concatenate([vb * hmv_ref[hh] for hh in range(QUAD)], axis=0)
            kdt = (kcs[qd, ch] * kdec_ref[qd]).T.astype(BF16)
            kvs[qd, ch] = _dot(kdt, vb)
    for qd in range(nquad):
        for ch in range(nch):
            kb = kcs[qd, ch].astype(BF16)
            kstack = jnp.concatenate([kb * hmk_ref[hh] for hh in range(QUAD)], axis=0)
            s_all = lax.dot_general(qcs[qd, ch].astype(BF16), kstack, (((1,), (1,)), ((), ())),
                                    preferred_element_type=F32)
            pmats[qd, ch] = (s_all * dtab_ref[qd]).astype(BF16)
    for qd in range(nquad):
        b0 = qd * qw
        state = r_ref[qd]
        for ch in range(nch):
            r0 = ch * CHUNK
            lhs = jnp.concatenate([pmats[qd, ch], (qcs[qd, ch] * qdec_ref[qd]).astype(BF16)], axis=1)
            rhs = jnp.concatenate([vstacks[qd, ch], state.astype(BF16)], axis=0)
            oret_ref[r0:r0 + CHUNK, b0:b0 + qw] = _dot(lhs, rhs)
            state = state * cd_ref[qd] + kvs[qd, ch] * bd_ref[...]
        r_ref[qd] = state

    amean = amean_ref[...]
    for qd in range(nquad):
        b0 = qd * qw
        o = oret_ref[:, b0:b0 + qw]
        hi = o.astype(BF16)
        lo = (o - hi.astype(F32)).astype(BF16)
        xc = o - (_dot(hi, amean) + _dot(lo, amean))
        var = _dot((xc * xc).astype(BF16), amean)
        yn = xc * lax.rsqrt(var + EPS) * gng_ref[:, b0:b0 + qw] + gnb_ref[:, b0:b0 + qw]
        gr = proj[:, 3 * rw + b0:3 * rw + b0 + qw]
        mix_ref[:, b0:b0 + qw] = (gr * jax.nn.sigmoid(gr) * yn).astype(BF16)

    mixed = _dot(mix_ref[...], wout_ref[...])
    o_ref[...] = h_ref[...] + _rms(mixed, postg_ref[...])


def _mixer_tables(s_len, rw, hd, q_scale):
    heads = rw // hd
    half = hd // 2
    qw = QUAD * hd
    inv_freq = ROPE_BASE ** (-jnp.arange(0, hd, 2, dtype=F32) / hd)
    ang = jnp.arange(s_len, dtype=F32)[:, None] * inv_freq[None, :]
    cos = jnp.tile(jnp.cos(ang), (1, QUAD))
    sin = jnp.tile(jnp.sin(ang), (1, QUAD))

    log_gamma = jnp.log1p(-jnp.exp2(-5.0 - jnp.arange(heads, dtype=F32)))
    pos = jnp.arange(CHUNK, dtype=F32)
    diff = pos[:, None] - pos[None, :]
    intra = jnp.where(diff[None] >= 0,
                      jnp.exp(log_gamma[:, None, None] * jnp.maximum(diff, 0.0)[None]), 0.0)
    k_decay = jnp.exp(log_gamma[None, :] * (CHUNK - 1.0 - pos)[:, None])
    q_decay = jnp.exp(log_gamma[None, :] * (pos + 1.0)[:, None])
    chunk_decay = jnp.exp(log_gamma * CHUNK)

    nquad = heads // QUAD
    lane = jnp.arange(qw)
    hk = (lane % (QUAD * half)) // half
    hv = lane // hd
    hid = jnp.arange(QUAD)
    hmk = jnp.broadcast_to((hk[None, :] == hid[:, None]).astype(BF16)[:, None, :],
                           (QUAD, CHUNK, qw))
    hmv = jnp.broadcast_to((hv[None, :] == hid[:, None]).astype(BF16)[:, None, :],
                           (QUAD, CHUNK, qw))
    dtab = q_scale * intra.reshape(nquad, QUAD, CHUNK, CHUNK).transpose(0, 2, 1, 3).reshape(
        nquad, CHUNK, QUAD * CHUNK)
    qh = jnp.arange(nquad)[:, None] * QUAD + hk[None, :]
    qdec = q_scale * jnp.transpose(q_decay[:, qh], (1, 0, 2))
    kdec = jnp.transpose(k_decay[:, qh], (1, 0, 2))
    bd = (hk[:, None] == hv[None, :]).astype(F32)
    cd = chunk_decay[qh][:, :, None] * bd[None]
    amean = ((hv[:, None] == hv[None, :]).astype(F32) / hd).astype(BF16)
    return cos, sin, hmk, hmv, dtab, qdec, kdec, cd, bd, amean


def _qk_perm(rw, hd):
    half = hd // 2
    qw = QUAD * hd
    c = jnp.arange(rw)
    quad, w = c // qw, c % qw
    part, hh, e = w // (QUAD * half), (w % (QUAD * half)) // half, w % half
    return (quad * QUAD + hh) * hd + part * half + e


def _mixer_call(h2, bsz, s_len, pre_g, w_in, gn_g, gn_b, conv_w, conv_b, ln_g, ln_b,
                w_out, post_g, ts):
    t, d = h2.shape
    rw = gn_g.shape[0]
    cw = conv_b.shape[0]
    assert rw == cw and rw % (QUAD * 2) == 0
    hd = rw // RET_HEADS
    assert ts % CHUNK == 0 and s_len % ts == 0
    nst = s_len // ts
    tabs = _mixer_tables(s_len, rw, hd, hd ** -0.5)
    cos, sin = tabs[0], tabs[1]
    perm = _qk_perm(rw, hd)
    w_in_p = jnp.concatenate(
        [w_in[:, 4 * rw:], w_in[:, :rw][:, perm], w_in[:, rw:2 * rw][:, perm], w_in[:, 2 * rw:4 * rw]],
        axis=1).astype(BF16)
    convw = jnp.broadcast_to(conv_w[:, None, :], (CONV_KERNEL, SUBLANES, cw))
    consts_a = [pre_g.reshape(1, d), w_in_p]
    consts_b = list(tabs[2:]) + [gn_g.reshape(1, rw), gn_b.reshape(1, rw), convw,
                                 conv_b.reshape(1, cw), ln_g.reshape(1, cw), ln_b.reshape(1, cw),
                                 w_out.astype(BF16), post_g.reshape(1, d)]
    row = lambda b, s: (b * nst + s, 0)
    tab = lambda b, s: (s, 0)
    specs = ([pl.BlockSpec((ts, d), row)] + [_const_spec(c.shape) for c in consts_a]
             + [pl.BlockSpec((ts, cos.shape[1]), tab), pl.BlockSpec((ts, sin.shape[1]), tab)]
             + [_const_spec(c.shape) for c in consts_b])
    qw = QUAD * hd
    hist = _conv_hist(ts)
    scratch = [
        pltpu.VMEM((rw // qw, qw, qw), F32),
        pltpu.VMEM((cw // LANES, hist + ts, LANES), F32),
        pltpu.VMEM((cw // LANES, hist + ts + (CONV_KERNEL - 1) * SUBLANES, LANES), F32),
        pltpu.VMEM((cw // LANES, hist + ts, LANES), F32),
        pltpu.VMEM((ts, rw), F32),
        pltpu.VMEM((ts, rw + cw), BF16),
    ]
    return pl.pallas_call(
        functools.partial(_mixer_kernel, ts=ts, rw=rw),
        out_shape=jax.ShapeDtypeStruct((t, d), F32),
        grid=(bsz, nst),
        in_specs=specs,
        out_specs=pl.BlockSpec((ts, d), row),
        scratch_shapes=scratch,
        compiler_params=pltpu.CompilerParams(
            dimension_semantics=("arbitrary", "arbitrary"), vmem_limit_bytes=V7X_VMEM_LIMIT),
        name="mixer",
    )(h2, *consts_a, cos, sin, *consts_b)


def _tile(n, want):
    t = min(n, want)
    while n % t:
        t //= 2
    return t


def kernel(x, p, ffn1_pre_g, ffn1_w_gu, ffn1_w_down, ffn1_post_g, mix_pre_g, w_in, ret_gn_g, ret_gn_b, conv_w, conv_b, conv_ln_g, conv_ln_b, w_out, mix_post_g, ffn2_pre_g, ffn2_w_gu, ffn2_w_down, ffn2_post_g, ple_w, ple_gate_norm_g, ple_gate_w, ple_post_g):
    bsz, s_len, d = x.shape
    t = bsz * s_len
    tm = _tile(t, 1024)
    ts = _tile(s_len, 1024)
    h = x.reshape(t, d)
    for i in range(p.shape[0]):
        h = _ffn_call(h, ffn1_pre_g[i], ffn1_w_gu[i], ffn1_w_down[i], ffn1_post_g[i], tm)
        h = _mixer_call(h, bsz, s_len, mix_pre_g[i], w_in[i], ret_gn_g[i], ret_gn_b[i],
                        conv_w[i], conv_b[i], conv_ln_g[i], conv_ln_b[i], w_out[i],
                        mix_post_g[i], ts)
        h = _ffn_call(h, ffn2_pre_g[i], ffn2_w_gu[i], ffn2_w_down[i], ffn2_post_g[i], tm,
                      ple=(p[i].reshape(t, p.shape[-1]), ple_w[i], ple_gate_norm_g[i],
                           ple_gate_w[i], ple_post_g[i]))
    return h.reshape(bsz, s_len, d)
```

```python
import functools

import jax
import jax.numpy as jnp
from jax import lax
from jax.experimental import pallas as pl
from jax.experimental.pallas import tpu as pltpu

EPS = 1e-6
ROPE_BASE = 10000.0
RET_HEADS = 8
CONV_KERNEL = 31
CHUNK = 128
QUAD = 4
SUBLANES = 8
LANES = 128
CONV_TILES = 8
FFN_ROWS = 512
V7X_VMEM_LIMIT = 56 * 1024 * 1024

BF16 = jnp.bfloat16
F32 = jnp.float32


def _rms(x, g):
    ms = jnp.mean(x * x, axis=-1, keepdims=True)
    return x * lax.rsqrt(ms + EPS) * g


def _dot(a, b):
    return jnp.dot(a, b, preferred_element_type=F32)


def _const_spec(shape):
    nd = len(shape)
    return pl.BlockSpec(shape, lambda *_: (0,) * nd, pipeline_mode=pl.Buffered(1))


def _ff_chunks(d_ff):
    step = 1024
    out, c = [], 0
    while c < d_ff:
        out.append((c, min(step, d_ff - c)))
        c += step
    return out


def _sub_tiles(tm):
    n = max(1, tm // FFN_ROWS)
    return [slice(i * (tm // n), (i + 1) * (tm // n)) for i in range(n)]


def _ffn_body(x_ref, preg_ref, wgu_ref, wd_ref, postg_ref, o_ref, d_ff, tail=None):
    subs = _sub_tiles(x_ref.shape[0])
    chunks = _ff_chunks(d_ff)
    work = [(s, c) for s in range(len(subs)) for c in range(len(chunks))]
    xs, xns, gus, fs = {}, {}, {}, {}

    def gate_up(s, c):
        if s not in xns:
            xs[s] = x_ref[subs[s], :]
            xns[s] = _rms(xs[s], preg_ref[...]).astype(BF16)
        c0, fc = chunks[c]
        gus[s, c] = (_dot(xns[s], wgu_ref[:, c0:c0 + fc]),
                     _dot(xns[s], wgu_ref[:, d_ff + c0:d_ff + c0 + fc]))

    def down(s, c):
        c0, fc = chunks[c]
        g, u = gus.pop((s, c))
        d = _dot((g * jax.nn.sigmoid(g) * u).astype(BF16), wd_ref[c0:c0 + fc, :])
        fs[s] = d if c == 0 else fs[s] + d

    def finish(s):
        h = xs[s] + 0.5 * _rms(fs.pop(s), postg_ref[...])
        o_ref[subs[s], :] = h if tail is None else tail(h, subs[s])

    pending = []
    gate_up(*work[0])
    for i, (s, c) in enumerate(work):
        if i + 1 < len(work):
            gate_up(*work[i + 1])
        down(s, c)
        pending = [(ps, n - 1) for ps, n in pending]
        for ps, n in pending:
            if n <= 0:
                finish(ps)
        pending = [(ps, n) for ps, n in pending if n > 0]
        if c == len(chunks) - 1:
            pending.append((s, 2))
    for ps, _ in pending:
        finish(ps)


def _ffn_kernel(x_ref, preg_ref, wgu_ref, wd_ref, postg_ref, o_ref, *, d_ff):
    _ffn_body(x_ref, preg_ref, wgu_ref, wd_ref, postg_ref, o_ref, d_ff)


def _ffn_ple_kernel(x_ref, p_ref, preg_ref, wgu_ref, wd_ref, postg_ref,
                    plew_ref, gng_ref, gw_ref, plepost_ref, o_ref, *, d_ff):
    def ple(h, rows):
        e = _dot(p_ref[rows, :].astype(BF16), plew_ref[...])
        gate = jax.nn.sigmoid(_dot(_rms(h, gng_ref[...]).astype(BF16), gw_ref[...]))
        return h + _rms(gate * e, plepost_ref[...])

    _ffn_body(x_ref, preg_ref, wgu_ref, wd_ref, postg_ref, o_ref, d_ff, tail=ple)


def _ffn_call(x2, pre_g, w_gu, w_down, post_g, tm, ple=None):
    t, d = x2.shape
    d_ff = w_down.shape[0]
    row = lambda i: (i, 0)
    args = [x2]
    specs = [pl.BlockSpec((tm, d), row)]
    if ple is not None:
        p2, ple_w, gn_g, gate_w, ple_post_g = ple
        args.append(p2)
        specs.append(pl.BlockSpec((tm, p2.shape[1]), row))
    consts = [pre_g.reshape(1, d), w_gu.astype(BF16), w_down.astype(BF16), post_g.reshape(1, d)]
    if ple is not None:
        consts += [ple_w.astype(BF16), gn_g.reshape(1, d), gate_w.astype(BF16),
                   ple_post_g.reshape(1, d)]
    args += consts
    specs += [_const_spec(c.shape) for c in consts]
    body = _ffn_kernel if ple is None else _ffn_ple_kernel
    return pl.pallas_call(
        functools.partial(body, d_ff=d_ff),
        out_shape=jax.ShapeDtypeStruct((t, d), F32),
        grid=(t // tm,),
        in_specs=specs,
        out_specs=pl.BlockSpec((tm, d), row),
        compiler_params=pltpu.CompilerParams(
            dimension_semantics=("arbitrary",), vmem_limit_bytes=V7X_VMEM_LIMIT),
        name="ffn" if ple is None else "ffn_ple",
    )(*args)


def _conv_hist(ts):
    hist = -(-(CONV_KERNEL - 1) // SUBLANES) * SUBLANES
    while ((hist + ts) // SUBLANES) % 2 == 0:
        hist += SUBLANES
    return hist


def _conv(c, b_ref, xp_ref, ys_ref, convw_ref, convb_ref, *, ts, cw):
    hist = _conv_hist(ts)
    seg = (hist + ts) // SUBLANES
    pad = CONV_KERNEL - 1
    ncol = cw // LANES
    for col in range(ncol):
        b_ref[col, hist:hist + ts, :] = c[col]
    for col in range(ncol):
        for i in range(seg):
            xp_ref[col, (pad + i) * SUBLANES:(pad + i + 1) * SUBLANES, :] = (
                b_ref[col, pl.ds(i, SUBLANES, stride=seg), :])
        for k in range(1, pad + 1):
            src = xp_ref[col, (pad + seg - k) * SUBLANES:(pad + seg - k + 1) * SUBLANES, :]
            xp_ref[col, (pad - k) * SUBLANES:(pad - k + 1) * SUBLANES, :] = pltpu.roll(src, 1, axis=0)
    for col in range(ncol):
        lanes = slice(col * LANES, (col + 1) * LANES)
        bias = convb_ref[:, lanes]
        for i0 in range(0, seg, CONV_TILES):
            g = min(CONV_TILES, seg - i0)
            acc = None
            for j in range(CONV_KERNEL):
                lo = (pad + i0 - (CONV_KERNEL - 1 - j)) * SUBLANES
                w = jnp.concatenate([convw_ref[j, :, lanes]] * g, axis=0)
                t = xp_ref[col, lo:lo + g * SUBLANES, :] * w
                acc = t if acc is None else acc + t
            acc = acc + bias
            for i in range(g):
                ys_ref[col, pl.ds(i0 + i, SUBLANES, stride=seg), :] = acc[i * SUBLANES:(i + 1) * SUBLANES]
    for col in range(ncol):
        b_ref[col, 0:hist, :] = b_ref[col, ts:ts + hist, :]
    return jnp.concatenate([ys_ref[col, hist:hist + ts, :] for col in range(ncol)], axis=1)


def _mixer_kernel(h_ref, preg_ref, win_ref, cos_ref, sin_ref, hmk_ref, hmv_ref,
                  dtab_ref, qdec_ref, kdec_ref, cd_ref, bd_ref, amean_ref,
                  gng_ref, gnb_ref, convw_ref, convb_ref, lng_ref, lnb_ref,
                  wout_ref, postg_ref, o_ref,
                  r_ref, b_ref, xp_ref, ys_ref, oret_ref, mix_ref,
                  *, ts, rw):
    cw = rw
    qw = rw // 2
    nquad = 2

    @pl.when(pl.program_id(1) == 0)
    def _():
        r_ref[...] = jnp.zeros_like(r_ref)
        b_ref[...] = jnp.zeros_like(b_ref)

    u = _rms(h_ref[...], preg_ref[...]).astype(BF16)

    pg = _dot(u, win_ref[:, 0:2 * cw])
    c = [pg[:, 2 * i * LANES:(2 * i + 1) * LANES] * jax.nn.sigmoid(pg[:, (2 * i + 1) * LANES:(2 * i + 2) * LANES])
         for i in range(cw // LANES)]
    proj = _dot(u, win_ref[:, 2 * cw:])
    y = _conv(c, b_ref, xp_ref, ys_ref, convw_ref, convb_ref, ts=ts, cw=cw)
    yc = y - jnp.mean(y, axis=-1, keepdims=True)
    yn = yc * lax.rsqrt(jnp.mean(yc * yc, axis=-1, keepdims=True) + EPS) * lng_ref[...] + lnb_ref[...]
    mix_ref[:, rw:rw + cw] = (yn * jax.nn.sigmoid(yn)).astype(BF16)

    cos = cos_ref[...]
    sin = sin_ref[...]
    nch = ts // CHUNK
    qcs, kcs, vstacks, kvs, pmats = {}, {}, {}, {}, {}
    for qd in range(nquad):
        b0 = qd * qw
        q1 = proj[:, b0:b0 + qw // 2]
        q2 = proj[:, b0 + qw // 2:b0 + qw]
        k1 = proj[:, rw + b0:rw + b0 + qw // 2]
        k2 = proj[:, rw + b0 + qw // 2:rw + b0 + qw]
        qr = jnp.concatenate([q1 * cos - q2 * sin, q1 * sin + q2 * cos], axis=1)
        kr = jnp.concatenate([k1 * cos - k2 * sin, k1 * sin + k2 * cos], axis=1)
        v = proj[:, 2 * rw + b0:2 * rw + b0 + qw]
        for ch in range(nch):
            r0 = ch * CHUNK
            qcs[qd, ch] = qr[r0:r0 + CHUNK]
            kcs[qd, ch] = kr[r0:r0 + CHUNK]
            vb = v[r0:r0 + CHUNK].astype(BF16)
            vstacks[qd, ch] = jnp.concatenate([vb * hmv_ref[hh] for hh in range(QUAD)], axis=0)
            kdt = (kcs[qd, ch] * kdec_ref[qd]).T.astype(BF16)
            kvs[qd, ch] = _dot(kdt, vb)
    for qd in range(nquad):
        for ch in range(nch):
            kb = kcs[qd, ch].astype(BF16)
            kstack = jnp.concatenate([kb * hmk_ref[hh] for hh in range(QUAD)], axis=0)
            s_all = lax.dot_general(qcs[qd, ch].astype(BF16), kstack, (((1,), (1,)), ((), ())),
                                    preferred_element_type=F32)
            pmats[qd, ch] = (s_all * dtab_ref[qd]).astype(BF16)
    for qd in range(nquad):
        b0 = qd * qw
        state = r_ref[qd]
        for ch in range(nch):
            r0 = ch * CHUNK
            lhs = jnp.concatenate([pmats[qd, ch], (qcs[qd, ch] * qdec_ref[qd]).astype(BF16)], axis=1)
            rhs = jnp.concatenate([vstacks[qd, ch], state.astype(BF16)], axis=0)
            oret_ref[r0:r0 + CHUNK, b0:b0 + qw] = _dot(lhs, rhs)
            state = state * cd_ref[qd] + kvs[qd, ch] * bd_ref[...]
        r_ref[qd] = state

    amean = amean_ref[...]
    for qd in range(nquad):
        b0 = qd * qw
        o = oret_ref[:, b0:b0 + qw]
        hi = o.astype(BF16)
        lo = (o - hi.astype(F32)).astype(BF16)
        xc = o - (_dot(hi, amean) + _dot(lo, amean))
        var = _dot((xc * xc).astype(BF16), amean)
        yn = xc * lax.rsqrt(var + EPS) * gng_ref[:, b0:b0 + qw] + gnb_ref[:, b0:b0 + qw]
        gr = proj[:, 3 * rw + b0:3 * rw + b0 + qw]
        mix_ref[:, b0:b0 + qw] = (gr * jax.nn.sigmoid(gr) * yn).astype(BF16)

    mixed = _dot(mix_ref[...], wout_ref[...])
    o_ref[...] = h_ref[...] + _rms(mixed, postg_ref[...])


def _mixer_tables(s_len, rw, hd, q_scale):
    heads = rw // hd
    half = hd // 2
    qw = QUAD * hd
    inv_freq = ROPE_BASE ** (-jnp.arange(0, hd, 2, dtype=F32) / hd)
    ang = jnp.arange(s_len, dtype=F32)[:, None] * inv_freq[None, :]
    cos = jnp.tile(jnp.cos(ang), (1, QUAD))
    sin = jnp.tile(jnp.sin(ang), (1, QUAD))

    log_gamma = jnp.log1p(-jnp.exp2(-5.0 - jnp.arange(heads, dtype=F32)))
    pos = jnp.arange(CHUNK, dtype=F32)
    diff = pos[:, None] - pos[None, :]
    intra = jnp.where(diff[None] >= 0,
                      jnp.exp(log_gamma[:, None, None] * jnp.maximum(diff, 0.0)[None]), 0.0)
    k_decay = jnp.exp(log_gamma[None, :] * (CHUNK - 1.0 - pos)[:, None])
    q_decay = jnp.exp(log_gamma[None, :] * (pos + 1.0)[:, None])
    chunk_decay = jnp.exp(log_gamma * CHUNK)

    nquad = heads // QUAD
    lane = jnp.arange(qw)
    hk = (lane % (QUAD * half)) // half
    hv = lane // hd
    hid = jnp.arange(QUAD)
    hmk = jnp.broadcast_to((hk[None, :] == hid[:, None]).astype(BF16)[:, None, :],
                           (QUAD, CHUNK, qw))
    hmv = jnp.broadcast_to((hv[None, :] == hid[:, None]).astype(BF16)[:, None, :],
                           (QUAD, CHUNK, qw))
    dtab = q_scale * intra.reshape(nquad, QUAD, CHUNK, CHUNK).transpose(0, 2, 1, 3).reshape(
        nquad, CHUNK, QUAD * CHUNK)
    qh = jnp.arange(nquad)[:, None] * QUAD + hk[None, :]
    qdec = q_scale * jnp.transpose(q_decay[:, qh], (1, 0, 2))
    kdec = jnp.transpose(k_decay[:, qh], (1, 0, 2))
    bd = (hk[:, None] == hv[None, :]).astype(F32)
    cd = chunk_decay[qh][:, :, None] * bd[None]
    amean = ((hv[:, None] == hv[None, :]).astype(F32) / hd).astype(BF16)
    return cos, sin, hmk, hmv, dtab, qdec, kdec, cd, bd, amean


def _qk_perm(rw, hd):
    half = hd // 2
    qw = QUAD * hd
    c = jnp.arange(rw)
    quad, w = c // qw, c % qw
    part, hh, e = w // (QUAD * half), (w % (QUAD * half)) // half, w % half
    return (quad * QUAD + hh) * hd + part * half + e


def _mixer_call(h2, bsz, s_len, pre_g, w_in, gn_g, gn_b, conv_w, conv_b, ln_g, ln_b,
                w_out, post_g, ts):
    t, d = h2.shape
    rw = gn_g.shape[0]
    cw = conv_b.shape[0]
    assert rw == cw and rw % (QUAD * 2) == 0
    hd = rw // RET_HEADS
    assert ts % CHUNK == 0 and s_len % ts == 0
    nst = s_len // ts
    tabs = _mixer_tables(s_len, rw, hd, hd ** -0.5)
    cos, sin = tabs[0], tabs[1]
    perm = _qk_perm(rw, hd)
    glu = w_in[:, 4 * rw:].reshape(d, 2, cw // LANES, LANES).transpose(0, 2, 1, 3).reshape(d, 2 * cw)
    w_in_p = jnp.concatenate(
        [glu, w_in[:, :rw][:, perm], w_in[:, rw:2 * rw][:, perm], w_in[:, 2 * rw:4 * rw]],
        axis=1).astype(BF16)
    convw = jnp.broadcast_to(conv_w[:, None, :], (CONV_KERNEL, SUBLANES, cw))
    consts_a = [pre_g.reshape(1, d), w_in_p]
    consts_b = list(tabs[2:]) + [gn_g.reshape(1, rw), gn_b.reshape(1, rw), convw,
                                 conv_b.reshape(1, cw), ln_g.reshape(1, cw), ln_b.reshape(1, cw),
                                 w_out.astype(BF16), post_g.reshape(1, d)]
    row = lambda b, s: (b * nst + s, 0)
    tab = lambda b, s: (s, 0)
    specs = ([pl.BlockSpec((ts, d), row)] + [_const_spec(c.shape) for c in consts_a]
             + [pl.BlockSpec((ts, cos.shape[1]), tab), pl.BlockSpec((ts, sin.shape[1]), tab)]
             + [_const_spec(c.shape) for c in consts_b])
    qw = QUAD * hd
    hist = _conv_hist(ts)
    scratch = [
        pltpu.VMEM((rw // qw, qw, qw), F32),
        pltpu.VMEM((cw // LANES, hist + ts, LANES), F32),
        pltpu.VMEM((cw // LANES, hist + ts + (CONV_KERNEL - 1) * SUBLANES, LANES), F32),
        pltpu.VMEM((cw // LANES, hist + ts, LANES), F32),
        pltpu.VMEM((ts, rw), F32),
        pltpu.VMEM((ts, rw + cw), BF16),
    ]
    return pl.pallas_call(
        functools.partial(_mixer_kernel, ts=ts, rw=rw),
        out_shape=jax.ShapeDtypeStruct((t, d), F32),
        grid=(bsz, nst),
        in_specs=specs,
        out_specs=pl.BlockSpec((ts, d), row),
        scratch_shapes=scratch,
        compiler_params=pltpu.CompilerParams(
            dimension_semantics=("arbitrary", "arbitrary"), vmem_limit_bytes=V7X_VMEM_LIMIT),
        name="mixer",
    )(h2, *consts_a, cos, sin, *consts_b)


def _tile(n, want):
    t = min(n, want)
    while n % t:
        t //= 2
    return t


def kernel(x, p, ffn1_pre_g, ffn1_w_gu, ffn1_w_down, ffn1_post_g, mix_pre_g, w_in, ret_gn_g, ret_gn_b, conv_w, conv_b, conv_ln_g, conv_ln_b, w_out, mix_post_g, ffn2_pre_g, ffn2_w_gu, ffn2_w_down, ffn2_post_g, ple_w, ple_gate_norm_g, ple_gate_w, ple_post_g):
    bsz, s_len, d = x.shape
    t = bsz * s_len
    tm = _tile(t, 1024)
    ts = _tile(s_len, 1024)
    h = x.reshape(t, d)
    for i in range(p.shape[0]):
        h = _ffn_call(h, ffn1_pre_g[i], ffn1_w_gu[i], ffn1_w_down[i], ffn1_post_g[i], tm)
        h = _mixer_call(h, bsz, s_len, mix_pre_g[i], w_in[i], ret_gn_g[i], ret_gn_b[i],
                        conv_w[i], conv_b[i], conv_ln_g[i], conv_ln_b[i], w_out[i],
                        mix_post_g[i], ts)
        h = _ffn_call(h, ffn2_pre_g[i], ffn2_w_gu[i], ffn2_w_down[i], ffn2_post_g[i], tm,
                      ple=(p[i].reshape(t, p.shape[-1]), ple_w[i], ple_gate_norm_g[i],
                           ple_gate_w[i], ple_post_g[i]))
    return h.reshape(bsz, s_len, d)
```

```python
import functools

import jax
import jax.numpy as jnp
from jax import lax
from jax.experimental import pallas as pl
from jax.experimental.pallas import tpu as pltpu

EPS = 1e-6
ROPE_BASE = 10000.0
RET_HEADS = 8
CONV_KERNEL = 31
CHUNK = 128
QUAD = 4
SUBLANES = 8
LANES = 128
CONV_TILES = 8
FFN_ROWS = 256
V7X_VMEM_LIMIT = 56 * 1024 * 1024

BF16 = jnp.bfloat16
F32 = jnp.float32


def _rms(x, g):
    ms = jnp.mean(x * x, axis=-1, keepdims=True)
    return x * lax.rsqrt(ms + EPS) * g


def _dot(a, b):
    return jnp.dot(a, b, preferred_element_type=F32)


def _const_spec(shape):
    nd = len(shape)
    return pl.BlockSpec(shape, lambda *_: (0,) * nd, pipeline_mode=pl.Buffered(1))


def _ff_chunks(d_ff):
    step = 1024
    out, c = [], 0
    while c < d_ff:
        out.append((c, min(step, d_ff - c)))
        c += step
    return out


def _sub_tiles(tm):
    n = max(1, tm // FFN_ROWS)
    return [slice(i * (tm // n), (i + 1) * (tm // n)) for i in range(n)]


def _ffn_body(x_ref, preg_ref, wgu_ref, wd_ref, postg_ref, o_ref, d_ff, tail=None):
    subs = _sub_tiles(x_ref.shape[0])
    chunks = _ff_chunks(d_ff)
    work = [(s, c) for s in range(len(subs)) for c in range(len(chunks))]
    xs, xns, gus, fs = {}, {}, {}, {}

    def gate_up(s, c):
        if s not in xns:
            xs[s] = x_ref[subs[s], :]
            xns[s] = _rms(xs[s], preg_ref[...]).astype(BF16)
        c0, fc = chunks[c]
        gus[s, c] = (_dot(xns[s], wgu_ref[:, c0:c0 + fc]),
                     _dot(xns[s], wgu_ref[:, d_ff + c0:d_ff + c0 + fc]))

    def down(s, c):
        c0, fc = chunks[c]
        g, u = gus.pop((s, c))
        d = _dot((g * jax.nn.sigmoid(g) * u).astype(BF16), wd_ref[c0:c0 + fc, :])
        fs[s] = d if c == 0 else fs[s] + d

    def finish(s):
        h = xs[s] + 0.5 * _rms(fs.pop(s), postg_ref[...])
        o_ref[subs[s], :] = h if tail is None else tail(h, subs[s])

    pending = []
    gate_up(*work[0])
    for i, (s, c) in enumerate(work):
        if i + 1 < len(work):
            gate_up(*work[i + 1])
        down(s, c)
        pending = [(ps, n - 1) for ps, n in pending]
        for ps, n in pending:
            if n <= 0:
                finish(ps)
        pending = [(ps, n) for ps, n in pending if n > 0]
        if c == len(chunks) - 1:
            pending.append((s, 2))
    for ps, _ in pending:
        finish(ps)


def _ffn_kernel(x_ref, preg_ref, wgu_ref, wd_ref, postg_ref, o_ref, *, d_ff):
    _ffn_body(x_ref, preg_ref, wgu_ref, wd_ref, postg_ref, o_ref, d_ff)


def _ffn_ple_kernel(x_ref, p_ref, preg_ref, wgu_ref, wd_ref, postg_ref,
                    plew_ref, gng_ref, gw_ref, plepost_ref, o_ref, *, d_ff):
    def ple(h, rows):
        e = _dot(p_ref[rows, :].astype(BF16), plew_ref[...])
        gate = jax.nn.sigmoid(_dot(_rms(h, gng_ref[...]).astype(BF16), gw_ref[...]))
        return h + _rms(gate * e, plepost_ref[...])

    _ffn_body(x_ref, preg_ref, wgu_ref, wd_ref, postg_ref, o_ref, d_ff, tail=ple)


def _ffn_call(x2, pre_g, w_gu, w_down, post_g, tm, ple=None):
    t, d = x2.shape
    d_ff = w_down.shape[0]
    row = lambda i: (i, 0)
    args = [x2]
    specs = [pl.BlockSpec((tm, d), row)]
    if ple is not None:
        p2, ple_w, gn_g, gate_w, ple_post_g = ple
        args.append(p2)
        specs.append(pl.BlockSpec((tm, p2.shape[1]), row))
    consts = [pre_g.reshape(1, d), w_gu.astype(BF16), w_down.astype(BF16), post_g.reshape(1, d)]
    if ple is not None:
        consts += [ple_w.astype(BF16), gn_g.reshape(1, d), gate_w.astype(BF16),
                   ple_post_g.reshape(1, d)]
    args += consts
    specs += [_const_spec(c.shape) for c in consts]
    body = _ffn_kernel if ple is None else _ffn_ple_kernel
    return pl.pallas_call(
        functools.partial(body, d_ff=d_ff),
        out_shape=jax.ShapeDtypeStruct((t, d), F32),
        grid=(t // tm,),
        in_specs=specs,
        out_specs=pl.BlockSpec((tm, d), row),
        compiler_params=pltpu.CompilerParams(
            dimension_semantics=("arbitrary",), vmem_limit_bytes=V7X_VMEM_LIMIT),
        name="ffn" if ple is None else "ffn_ple",
    )(*args)


def _conv_hist(ts):
    hist = -(-(CONV_KERNEL - 1) // SUBLANES) * SUBLANES
    while ((hist + ts) // SUBLANES) % 2 == 0:
        hist += SUBLANES
    return hist


def _conv(c, b_ref, xp_ref, ys_ref, convw_ref, convb_ref, *, ts, cw):
    hist = _conv_hist(ts)
    seg = (hist + ts) // SUBLANES
    pad = CONV_KERNEL - 1
    ncol = cw // LANES
    for col in range(ncol):
        b_ref[col, hist:hist + ts, :] = c[col]
    for col in range(ncol):
        for i in range(seg):
            xp_ref[col, (pad + i) * SUBLANES:(pad + i + 1) * SUBLANES, :] = (
                b_ref[col, pl.ds(i, SUBLANES, stride=seg), :])
        for k in range(1, pad + 1):
            src = xp_ref[col, (pad + seg - k) * SUBLANES:(pad + seg - k + 1) * SUBLANES, :]
            xp_ref[col, (pad - k) * SUBLANES:(pad - k + 1) * SUBLANES, :] = pltpu.roll(src, 1, axis=0)
    for col in range(ncol):
        lanes = slice(col * LANES, (col + 1) * LANES)
        bias = convb_ref[:, lanes]
        for i0 in range(0, seg, CONV_TILES):
            g = min(CONV_TILES, seg - i0)
            acc = None
            for j in range(CONV_KERNEL):
                lo = (pad + i0 - (CONV_KERNEL - 1 - j)) * SUBLANES
                w = jnp.concatenate([convw_ref[j, :, lanes]] * g, axis=0)
                t = xp_ref[col, lo:lo + g * SUBLANES, :] * w
                acc = t if acc is None else acc + t
            acc = acc + bias
            for i in range(g):
                ys_ref[col, pl.ds(i0 + i, SUBLANES, stride=seg), :] = acc[i * SUBLANES:(i + 1) * SUBLANES]
    for col in range(ncol):
        b_ref[col, 0:hist, :] = b_ref[col, ts:ts + hist, :]
    return jnp.concatenate([ys_ref[col, hist:hist + ts, :] for col in range(ncol)], axis=1)


def _mixer_kernel(h_ref, preg_ref, win_ref, cos_ref, sin_ref, hmk_ref, hmv_ref,
                  dtab_ref, qdec_ref, kdec_ref, cd_ref, bd_ref, amean_ref,
                  gng_ref, gnb_ref, convw_ref, convb_ref, lng_ref, lnb_ref,
                  wout_ref, postg_ref, o_ref,
                  r_ref, b_ref, xp_ref, ys_ref, oret_ref, mix_ref,
                  *, ts, rw):
    cw = rw
    qw = rw // 2
    nquad = 2

    @pl.when(pl.program_id(1) == 0)
    def _():
        r_ref[...] = jnp.zeros_like(r_ref)
        b_ref[...] = jnp.zeros_like(b_ref)

    u = _rms(h_ref[...], preg_ref[...]).astype(BF16)

    pg = _dot(u, win_ref[:, 0:2 * cw])
    c = [pg[:, 2 * i * LANES:(2 * i + 1) * LANES] * jax.nn.sigmoid(pg[:, (2 * i + 1) * LANES:(2 * i + 2) * LANES])
         for i in range(cw // LANES)]
    proj = _dot(u, win_ref[:, 2 * cw:])
    y = _conv(c, b_ref, xp_ref, ys_ref, convw_ref, convb_ref, ts=ts, cw=cw)
    yc = y - jnp.mean(y, axis=-1, keepdims=True)
    yn = yc * lax.rsqrt(jnp.mean(yc * yc, axis=-1, keepdims=True) + EPS) * lng_ref[...] + lnb_ref[...]
    mix_ref[:, rw:rw + cw] = (yn * jax.nn.sigmoid(yn)).astype(BF16)

    cos = cos_ref[...]
    sin = sin_ref[...]
    nch = ts // CHUNK
    qcs, kcs, vstacks, kvs, pmats = {}, {}, {}, {}, {}
    for qd in range(nquad):
        b0 = qd * qw
        q1 = proj[:, b0:b0 + qw // 2]
        q2 = proj[:, b0 + qw // 2:b0 + qw]
        k1 = proj[:, rw + b0:rw + b0 + qw // 2]
        k2 = proj[:, rw + b0 + qw // 2:rw + b0 + qw]
        qr = jnp.concatenate([q1 * cos - q2 * sin, q1 * sin + q2 * cos], axis=1)
        kr = jnp.concatenate([k1 * cos - k2 * sin, k1 * sin + k2 * cos], axis=1)
        v = proj[:, 2 * rw + b0:2 * rw + b0 + qw]
        for ch in range(nch):
            r0 = ch * CHUNK
            qcs[qd, ch] = qr[r0:r0 + CHUNK]
            kcs[qd, ch] = kr[r0:r0 + CHUNK]
            vb = v[r0:r0 + CHUNK].astype(BF16)
            vstacks[qd, ch] = jnp.concatenate([vb * hmv_ref[hh] for hh in range(QUAD)], axis=0)
            kdt = (kcs[qd, ch] * kdec_ref[qd]).astype(BF16).T
            kvs[qd, ch] = _dot(kdt, vb)
    for qd in range(nquad):
        for ch in range(nch):
            kb = kcs[qd, ch].astype(BF16)
            kstack = jnp.concatenate([kb * hmk_ref[hh] for hh in range(QUAD)], axis=0)
            s_all = lax.dot_general(qcs[qd, ch].astype(BF16), kstack, (((1,), (1,)), ((), ())),
                                    preferred_element_type=F32)
            pmats[qd, ch] = (s_all * dtab_ref[qd]).astype(BF16)
    for qd in range(nquad):
        b0 = qd * qw
        state = r_ref[qd]
        for ch in range(nch):
            r0 = ch * CHUNK
            lhs = jnp.concatenate([pmats[qd, ch], (qcs[qd, ch] * qdec_ref[qd]).astype(BF16)], axis=1)
            rhs = jnp.concatenate([vstacks[qd, ch], state.astype(BF16)], axis=0)
            oret_ref[r0:r0 + CHUNK, b0:b0 + qw] = _dot(lhs, rhs)
            state = state * cd_ref[qd] + kvs[qd, ch] * bd_ref[...]
        r_ref[qd] = state

    amean = amean_ref[...]
    for qd in range(nquad):
        b0 = qd * qw
        o = oret_ref[:, b0:b0 + qw]
        hi = o.astype(BF16)
        lo = (o - hi.astype(F32)).astype(BF16)
        xc = o - (_dot(hi, amean) + _dot(lo, amean))
        var = _dot((xc * xc).astype(BF16), amean)
        yn = xc * lax.rsqrt(var + EPS) * gng_ref[:, b0:b0 + qw] + gnb_ref[:, b0:b0 + qw]
        gr = proj[:, 3 * rw + b0:3 * rw + b0 + qw]
        mix_ref[:, b0:b0 + qw] = (gr * jax.nn.sigmoid(gr) * yn).astype(BF16)

    mixed = _dot(mix_ref[...], wout_ref[...])
    o_ref[...] = h_ref[...] + _rms(mixed, postg_ref[...])


def _mixer_tables(s_len, rw, hd, q_scale):
    heads = rw // hd
    half = hd // 2
    qw = QUAD * hd
    inv_freq = ROPE_BASE ** (-jnp.arange(0, hd, 2, dtype=F32) / hd)
    ang = jnp.arange(s_len, dtype=F32)[:, None] * inv_freq[None, :]
    cos = jnp.tile(jnp.cos(ang), (1, QUAD))
    sin = jnp.tile(jnp.sin(ang), (1, QUAD))

    log_gamma = jnp.log1p(-jnp.exp2(-5.0 - jnp.arange(heads, dtype=F32)))
    pos = jnp.arange(CHUNK, dtype=F32)
    diff = pos[:, None] - pos[None, :]
    intra = jnp.where(diff[None] >= 0,
                      jnp.exp(log_gamma[:, None, None] * jnp.maximum(diff, 0.0)[None]), 0.0)
    k_decay = jnp.exp(log_gamma[None, :] * (CHUNK - 1.0 - pos)[:, None])
    q_decay = jnp.exp(log_gamma[None, :] * (pos + 1.0)[:, None])
    chunk_decay = jnp.exp(log_gamma * CHUNK)

    nquad = heads // QUAD
    lane = jnp.arange(qw)
    hk = (lane % (QUAD * half)) // half
    hv = lane // hd
    hid = jnp.arange(QUAD)
    hmk = jnp.broadcast_to((hk[None, :] == hid[:, None]).astype(BF16)[:, None, :],
                           (QUAD, CHUNK, qw))
    hmv = jnp.broadcast_to((hv[None, :] == hid[:, None]).astype(BF16)[:, None, :],
                           (QUAD, CHUNK, qw))
    dtab = q_scale * intra.reshape(nquad, QUAD, CHUNK, CHUNK).transpose(0, 2, 1, 3).reshape(
        nquad, CHUNK, QUAD * CHUNK)
    qh = jnp.arange(nquad)[:, None] * QUAD + hk[None, :]
    qdec = q_scale * jnp.transpose(q_decay[:, qh], (1, 0, 2))
    kdec = jnp.transpose(k_decay[:, qh], (1, 0, 2))
    bd = (hk[:, None] == hv[None, :]).astype(F32)
    cd = chunk_decay[qh][:, :, None] * bd[None]
    amean = ((hv[:, None] == hv[None, :]).astype(F32) / hd).astype(BF16)
    return cos, sin, hmk, hmv, dtab, qdec, kdec, cd, bd, amean


def _qk_perm(rw, hd):
    half = hd // 2
    qw = QUAD * hd
    c = jnp.arange(rw)
    quad, w = c // qw, c % qw
    part, hh, e = w // (QUAD * half), (w % (QUAD * half)) // half, w % half
    return (quad * QUAD + hh) * hd + part * half + e


def _mixer_call(h2, bsz, s_len, pre_g, w_in, gn_g, gn_b, conv_w, conv_b, ln_g, ln_b,
                w_out, post_g, ts):
    t, d = h2.shape
    rw = gn_g.shape[0]
    cw = conv_b.shape[0]
    assert rw == cw and rw % (QUAD * 2) == 0
    hd = rw // RET_HEADS
    assert ts % CHUNK == 0 and s_len % ts == 0
    nst = s_len // ts
    tabs = _mixer_tables(s_len, rw, hd, hd ** -0.5)
    cos, sin = tabs[0], tabs[1]
    perm = _qk_perm(rw, hd)
    glu = w_in[:, 4 * rw:].reshape(d, 2, cw // LANES, LANES).transpose(0, 2, 1, 3).reshape(d, 2 * cw)
    w_in_p = jnp.concatenate(
        [glu, w_in[:, :rw][:, perm], w_in[:, rw:2 * rw][:, perm], w_in[:, 2 * rw:4 * rw]],
        axis=1).astype(BF16)
    convw = jnp.broadcast_to(conv_w[:, None, :], (CONV_KERNEL, SUBLANES, cw))
    consts_a = [pre_g.reshape(1, d), w_in_p]
    consts_b = list(tabs[2:]) + [gn_g.reshape(1, rw), gn_b.reshape(1, rw), convw,
                                 conv_b.reshape(1, cw), ln_g.reshape(1, cw), ln_b.reshape(1, cw),
                                 w_out.astype(BF16), post_g.reshape(1, d)]
    row = lambda b, s: (b * nst + s, 0)
    tab = lambda b, s: (s, 0)
    specs = ([pl.BlockSpec((ts, d), row)] + [_const_spec(c.shape) for c in consts_a]
             + [pl.BlockSpec((ts, cos.shape[1]), tab), pl.BlockSpec((ts, sin.shape[1]), tab)]
             + [_const_spec(c.shape) for c in consts_b])
    qw = QUAD * hd
    hist = _conv_hist(ts)
    scratch = [
        pltpu.VMEM((rw // qw, qw, qw), F32),
        pltpu.VMEM((cw // LANES, hist + ts, LANES), F32),
        pltpu.VMEM((cw // LANES, hist + ts + (CONV_KERNEL - 1) * SUBLANES, LANES), F32),
        pltpu.VMEM((cw // LANES, hist + ts, LANES), F32),
        pltpu.VMEM((ts, rw), F32),
        pltpu.VMEM((ts, rw + cw), BF16),
    ]
    return pl.pallas_call(
        functools.partial(_mixer_kernel, ts=ts, rw=rw),
        out_shape=jax.ShapeDtypeStruct((t, d), F32),
        grid=(bsz, nst),
        in_specs=specs,
        out_specs=pl.BlockSpec((ts, d), row),
        scratch_shapes=scratch,
        compiler_params=pltpu.CompilerParams(
            dimension_semantics=("arbitrary", "arbitrary"), vmem_limit_bytes=V7X_VMEM_LIMIT),
        name="mixer",
    )(h2, *consts_a, cos, sin, *consts_b)


def _tile(n, want):
    t = min(n, want)
    while n % t:
        t //= 2
    return t


def kernel(x, p, ffn1_pre_g, ffn1_w_gu, ffn1_w_down, ffn1_post_g, mix_pre_g, w_in, ret_gn_g, ret_gn_b, conv_w, conv_b, conv_ln_g, conv_ln_b, w_out, mix_post_g, ffn2_pre_g, ffn2_w_gu, ffn2_w_down, ffn2_post_g, ple_w, ple_gate_norm_g, ple_gate_w, ple_post_g):
    bsz, s_len, d = x.shape
    t = bsz * s_len
    tm = _tile(t, 1024)
    ts = _tile(s_len, 1024)
    h = x.reshape(t, d)
    for i in range(p.shape[0]):
        h = _ffn_call(h, ffn1_pre_g[i], ffn1_w_gu[i], ffn1_w_down[i], ffn1_post_g[i], tm)
        h = _mixer_call(h, bsz, s_len, mix_pre_g[i], w_in[i], ret_gn_g[i], ret_gn_b[i],
                        conv_w[i], conv_b[i], conv_ln_g[i], conv_ln_b[i], w_out[i],
                        mix_post_g[i], ts)
        h = _ffn_call(h, ffn2_pre_g[i], ffn2_w_gu[i], ffn2_w_down[i], ffn2_post_g[i], tm,
                      ple=(p[i].reshape(t, p.shape[-1]), ple_w[i], ple_gate_norm_g[i],
                           ple_gate_w[i], ple_post_g[i]))
    return h.reshape(bsz, s_len, d)
```
